```python
import jax, jax.numpy as jnp
from jax import lax
import numpy as np

D_MODEL = 1024
BATCH = 2
SEQ = 8192
DEPTH = 1
DEC_BATCH = 128
DEC_SEQ = 8
PAST_LEN = 16384
PAGE_SIZE = 128

D_LRU = D_MODEL // 2
LRU_BLOCKS = 8
LRU_BLOCK = D_LRU // LRU_BLOCKS
CONV_W = 4
LRU_C = 8.0
MLA_HEADS = 8
QK_NOPE = 64
QK_ROPE = 32
V_DIM = 64
Q_LORA = D_MODEL // 4
KV_LORA = D_MODEL // 8
D_MLA = MLA_HEADS * V_DIM
D_MIX = D_LRU + D_MLA
D_IN = 2 * D_LRU + Q_LORA + KV_LORA + QK_ROPE
D_FF = 2816
ROPE_BASE = 10000.0
Q_BLOCK = 128
EPS = 1e-6

kernel_name = 'hymba_rglru_mla_macaron_step'


def rmsnorm(x, g):
    xf = x.astype(jnp.float32)
    y = xf * lax.rsqrt(jnp.mean(xf * xf, axis=-1, keepdims=True) + EPS)
    return (y * g.astype(jnp.float32)).astype(x.dtype)


def swiglu(x, w_gate, w_up, w_down):
    return (jax.nn.silu(x @ w_gate) * (x @ w_up)) @ w_down


def rope(x, pos):
    half = x.shape[-1] // 2
    freqs = ROPE_BASE ** (-jnp.arange(half, dtype=jnp.float32) / half)
    ang = pos.astype(jnp.float32)[:, None] * freqs[None, :]
    cos = jnp.cos(ang)[None, :, None, :]
    sin = jnp.sin(ang)[None, :, None, :]
    xf = x.astype(jnp.float32)
    x1, x2 = xf[..., :half], xf[..., half:]
    return jnp.concatenate([x1 * cos - x2 * sin, x2 * cos + x1 * sin], axis=-1).astype(x.dtype)


def causal_conv(x, buf, w, b):
    S = x.shape[1]
    xp = jnp.concatenate([buf.astype(x.dtype), x], axis=1)
    y = b + w[0] * xp[:, 0:S]
    for k in range(1, CONV_W):
        y = y + w[k] * xp[:, k:k + S]
    return y, xp[:, xp.shape[1] - (CONV_W - 1):]


def rg_lru(x, h0, w_a, b_a, w_i, b_i, lam):
    B, S, _ = x.shape
    xb = x.reshape(B, S, LRU_BLOCKS, LRU_BLOCK)
    r = jax.nn.sigmoid(jnp.einsum('bsnd,nde->bsne', xb, w_a).reshape(B, S, D_LRU) + b_a)
    i = jax.nn.sigmoid(jnp.einsum('bsnd,nde->bsne', xb, w_i).reshape(B, S, D_LRU) + b_i)
    log_a = -LRU_C * r.astype(jnp.float32) * jax.nn.softplus(-lam.astype(jnp.float32))
    a = jnp.exp(log_a)
    u = jnp.sqrt(-jnp.expm1(2.0 * log_a)) * (i * x).astype(jnp.float32)

    def step(h, au):
        a_t, u_t = au
        h = a_t * h + u_t
        return h, h

    h_last, hs = lax.scan(step, h0.astype(jnp.float32), (a.swapaxes(0, 1), u.swapaxes(0, 1)))
    return hs.swapaxes(0, 1).astype(x.dtype), h_last.astype(h0.dtype)


def mla_attend(q_lat, q_rope, q_pos, c_kv, k_rope, k_pos):
    B, S, H, C = q_lat.shape
    R = q_rope.shape[-1]
    qb = min(Q_BLOCK, S)
    nqb = S // qb
    scale = (QK_NOPE + QK_ROPE) ** -0.5

    def block(args):
        ql, qr, qp = args
        s = (jnp.einsum('bqhc,btc->bhqt', ql, c_kv)
             + jnp.einsum('bqhr,btr->bhqt', qr, k_rope)).astype(jnp.float32) * scale
        mask = k_pos[None, :] <= qp[:, None]
        s = jnp.where(mask[None, None], s, -jnp.inf)
        p = jax.nn.softmax(s, axis=-1).astype(c_kv.dtype)
        return jnp.einsum('bhqt,btc->bqhc', p, c_kv)

    xs = (q_lat.reshape(B, nqb, qb, H, C).swapaxes(0, 1),
          q_rope.reshape(B, nqb, qb, H, R).swapaxes(0, 1),
          q_pos.reshape(nqb, qb))
    o = lax.map(block, xs)
    return o.swapaxes(0, 1).reshape(B, S, H, C)


def mixer(hn, pos, past_c, past_kr, h0, conv0, p):
    B, S, _ = hn.shape
    z = hn @ p['w_in']
    o1 = D_LRU
    o2 = 2 * D_LRU
    o3 = o2 + Q_LORA
    o4 = o3 + KV_LORA
    x_l, g_l, c_q, c_kv, k_r = z[..., :o1], z[..., o1:o2], z[..., o2:o3], z[..., o3:o4], z[..., o4:]
    xc, conv_new = causal_conv(x_l, conv0, p['conv_w'], p['conv_b'])
    hs, h_new = rg_lru(xc, h0, p['lru_w_a'], p['lru_b_a'], p['lru_w_i'], p['lru_b_i'], p['lru_lambda'])
    y_lru = jax.nn.gelu(g_l) * hs
    q = (rmsnorm(c_q, p['q_norm']) @ p['w_q_up']).reshape(B, S, MLA_HEADS, QK_NOPE + QK_ROPE)
    q_nope = q[..., :QK_NOPE]
    q_rope = rope(q[..., QK_NOPE:], pos)
    c_kv = rmsnorm(c_kv, p['kv_norm'])
    k_r = rope(k_r[:, :, None, :], pos)[:, :, 0, :]
    q_lat = jnp.einsum('bshn,chn->bshc', q_nope, p['w_uk'])
    c_all = jnp.concatenate([past_c.astype(c_kv.dtype), c_kv], axis=1)
    kr_all = jnp.concatenate([past_kr.astype(k_r.dtype), k_r], axis=1)
    k_pos = jnp.arange(c_all.shape[1], dtype=jnp.int32)
    o_lat = mla_attend(q_lat, q_rope, pos, c_all, kr_all, k_pos)
    y_mla = jnp.einsum('bshc,chv->bshv', o_lat, p['w_uv']).reshape(B, S, D_MLA)
    y = jnp.concatenate([rmsnorm(y_lru, p['out_norm_lru']), rmsnorm(y_mla, p['out_norm_mla'])], axis=-1)
    return y @ p['w_out'], c_kv, k_r, h_new, conv_new


def layer(x, pos, past_c, past_kr, h0, conv0, p):
    x = x + 0.5 * swiglu(rmsnorm(x, p['ffn1_norm']), p['ffn1_w_gate'], p['ffn1_w_up'], p['ffn1_w_down'])
    y, c_new, kr_new, h_new, conv_new = mixer(rmsnorm(x, p['mix_norm']), pos, past_c, past_kr, h0, conv0, p)
    x = x + y
    x = x + 0.5 * swiglu(rmsnorm(x, p['ffn2_norm']), p['ffn2_w_gate'], p['ffn2_w_up'], p['ffn2_w_down'])
    return x, c_new, kr_new, h_new, conv_new


def setup_inputs(seed: int = 0) -> dict:
    key = jax.random.key(seed)
    ks = iter(jax.random.split(key, 40))
    n_pages = PAST_LEN // PAGE_SIZE
    n_used = DEC_BATCH * n_pages
    n_phys = (n_used * 5) // 4

    def nrm(shape, scale=1.0):
        return jax.random.normal(next(ks), shape, jnp.float32) * scale

    def gain(shape):
        return 1.0 + 0.05 * jax.random.normal(next(ks), shape, jnp.float32)

    L = DEPTH
    u = jax.random.uniform(next(ks), (L, D_LRU), jnp.float32, 0.9, 0.999)
    s = u ** (1.0 / LRU_C)
    lam = jnp.log(s) - jnp.log1p(-s)
    page_table = jax.random.permutation(next(ks), n_phys)[:n_used].reshape(DEC_BATCH, n_pages).astype(jnp.int32)
    return {
        'x_prompt': nrm((BATCH, SEQ, D_MODEL)),
        'x_sample': nrm((DEC_BATCH, DEC_SEQ, D_MODEL)),
        'cache_kv_latent': nrm((L, n_phys, PAGE_SIZE, KV_LORA)),
        'cache_k_rope': nrm((L, n_phys, PAGE_SIZE, QK_ROPE)),
        'state_lru_h': nrm((L, DEC_BATCH, D_LRU), 0.5),
        'state_conv': nrm((L, DEC_BATCH, CONV_W - 1, D_LRU)),
        'page_table': page_table,
        'ffn1_norm': gain((L, D_MODEL)),
        'ffn1_w_gate': nrm((L, D_MODEL, D_FF), D_MODEL ** -0.5),
        'ffn1_w_up': nrm((L, D_MODEL, D_FF), D_MODEL ** -0.5),
        'ffn1_w_down': nrm((L, D_FF, D_MODEL), D_FF ** -0.5),
        'mix_norm': gain((L, D_MODEL)),
        'w_in': nrm((L, D_MODEL, D_IN), D_MODEL ** -0.5),
        'conv_w': nrm((L, CONV_W, D_LRU), CONV_W ** -0.5),
        'conv_b': nrm((L, D_LRU), 0.02),
        'lru_w_a': nrm((L, LRU_BLOCKS, LRU_BLOCK, LRU_BLOCK), LRU_BLOCK ** -0.5),
        'lru_b_a': nrm((L, D_LRU), 0.02),
        'lru_w_i': nrm((L, LRU_BLOCKS, LRU_BLOCK, LRU_BLOCK), LRU_BLOCK ** -0.5),
        'lru_b_i': nrm((L, D_LRU), 0.02),
        'lru_lambda': lam,
        'q_norm': gain((L, Q_LORA)),
        'w_q_up': nrm((L, Q_LORA, MLA_HEADS * (QK_NOPE + QK_ROPE)), Q_LORA ** -0.5),
        'kv_norm': gain((L, KV_LORA)),
        'w_uk': nrm((L, KV_LORA, MLA_HEADS, QK_NOPE), KV_LORA ** -0.5),
        'w_uv': nrm((L, KV_LORA, MLA_HEADS, V_DIM), KV_LORA ** -0.5),
        'out_norm_lru': gain((L, D_LRU)),
        'out_norm_mla': gain((L, D_MLA)),
        'w_out': nrm((L, D_MIX, D_MODEL), D_MIX ** -0.5),
        'ffn2_norm': gain((L, D_MODEL)),
        'ffn2_w_gate': nrm((L, D_MODEL, D_FF), D_MODEL ** -0.5),
        'ffn2_w_up': nrm((L, D_MODEL, D_FF), D_MODEL ** -0.5),
        'ffn2_w_down': nrm((L, D_FF, D_MODEL), D_FF ** -0.5),
        'final_norm': gain((D_MODEL,)),
    }


def reference(x_prompt, x_sample, cache_kv_latent, cache_k_rope, state_lru_h, state_conv, page_table,
              ffn1_norm, ffn1_w_gate, ffn1_w_up, ffn1_w_down, mix_norm, w_in, conv_w, conv_b,
              lru_w_a, lru_b_a, lru_w_i, lru_b_i, lru_lambda, q_norm, w_q_up, kv_norm, w_uk, w_uv,
              out_norm_lru, out_norm_mla, w_out, ffn2_norm, ffn2_w_gate, ffn2_w_up, ffn2_w_down, final_norm):
    B, S = x_prompt.shape[0], x_prompt.shape[1]
    DB, DS = x_sample.shape[0], x_sample.shape[1]
    past_len = page_table.shape[1] * PAGE_SIZE
    pos_p = jnp.arange(S, dtype=jnp.int32)
    pos_s = past_len + jnp.arange(DS, dtype=jnp.int32)
    yp, ys = x_prompt, x_sample
    kvp, krp, hp, cvp, kvs, krs, hs_, cvs = [], [], [], [], [], [], [], []
    for l in range(DEPTH):
        p = {
            'ffn1_norm': ffn1_norm[l], 'ffn1_w_gate': ffn1_w_gate[l], 'ffn1_w_up': ffn1_w_up[l],
            'ffn1_w_down': ffn1_w_down[l], 'mix_norm': mix_norm[l], 'w_in': w_in[l],
            'conv_w': conv_w[l], 'conv_b': conv_b[l], 'lru_w_a': lru_w_a[l], 'lru_b_a': lru_b_a[l],
            'lru_w_i': lru_w_i[l], 'lru_b_i': lru_b_i[l], 'lru_lambda': lru_lambda[l],
            'q_norm': q_norm[l], 'w_q_up': w_q_up[l], 'kv_norm': kv_norm[l], 'w_uk': w_uk[l],
            'w_uv': w_uv[l], 'out_norm_lru': out_norm_lru[l], 'out_norm_mla': out_norm_mla[l],
            'w_out': w_out[l], 'ffn2_norm': ffn2_norm[l], 'ffn2_w_gate': ffn2_w_gate[l],
            'ffn2_w_up': ffn2_w_up[l], 'ffn2_w_down': ffn2_w_down[l],
        }
        zc = jnp.zeros((B, 0, KV_LORA), x_prompt.dtype)
        zr = jnp.zeros((B, 0, QK_ROPE), x_prompt.dtype)
        zh = jnp.zeros((B, D_LRU), jnp.float32)
        zv = jnp.zeros((B, CONV_W - 1, D_LRU), x_prompt.dtype)
        yp, c1, r1, h1, v1 = layer(yp, pos_p, zc, zr, zh, zv, p)
        past_c = cache_kv_latent[l, page_table].reshape(DB, past_len, KV_LORA)
        past_kr = cache_k_rope[l, page_table].reshape(DB, past_len, QK_ROPE)
        ys, c2, r2, h2, v2 = layer(ys, pos_s, past_c, past_kr, state_lru_h[l], state_conv[l], p)
        kvp.append(c1); krp.append(r1); hp.append(h1); cvp.append(v1)
        kvs.append(c2); krs.append(r2); hs_.append(h2); cvs.append(v2)
    yp = rmsnorm(yp, final_norm)
    ys = rmsnorm(ys, final_norm)
    return (yp, ys, jnp.stack(kvp), jnp.stack(krp), jnp.stack(hp), jnp.stack(cvp),
            jnp.stack(kvs), jnp.stack(krs), jnp.stack(hs_), jnp.stack(cvs))
```

```python
import functools

import numpy as np
import jax
import jax.numpy as jnp
from jax import lax
from jax.experimental import pallas as pl
from jax.experimental.pallas import tpu as pltpu

LRU_BLOCKS = 8
CONV_W = 4
LRU_C = 8.0
MLA_HEADS = 8
QK_NOPE = 64
QK_ROPE = 32
V_DIM = 64
ROPE_BASE = 10000.0
EPS = 1e-6
PAGE_SIZE = 128

LANES = 128
SUBLANES = 8
QK_PAD = 2 * LANES
VMEM_LIMIT = 56 * 1024 * 1024

F32 = jnp.float32
BF16 = jnp.bfloat16


def _rms(x, g):
    return x * lax.rsqrt(jnp.mean(x * x, axis=-1, keepdims=True) + EPS) * g


def _sigmoid(x):
    return 1.0 / (1.0 + jnp.exp(-x))


def _gelu_tanh(x):
    c = np.sqrt(2.0 / np.pi).astype(np.float32)
    return 0.5 * x * (1.0 + jnp.tanh(c * (x + 0.044715 * (x * x * x))))


def _softplus(x):
    return jnp.maximum(x, 0.0) + jnp.log1p(jnp.exp(-jnp.abs(x)))


def _dot(a, b):
    return jnp.dot(a, b, preferred_element_type=F32)


def _swiglu(xn, wg_ref, wu_ref, wd_ref):
    g = _dot(xn, wg_ref[...])
    u = _dot(xn, wu_ref[...])
    h = (g * _sigmoid(g)) * u
    return _dot(h.astype(BF16), wd_ref[...])


def _const_spec(shape):
    nd = len(shape)
    return pl.BlockSpec(shape, lambda *_: (0,) * nd, pipeline_mode=pl.Buffered(1))


def _front_kernel(x_ref, n1_ref, wg_ref, wu_ref, wd_ref, nm_ref, win_ref, qn_ref, wq_ref, kvn_ref,
                  wuk_ref, cos_ref, sin_ref,
                  x1_ref, xg_ref, q_ref, kcat_ref, kv_ref, kr_ref, *, d_lru, q_lora, kv_lora, scale):
    x = x_ref[...]
    xn = _rms(x, n1_ref[...]).astype(BF16)
    x1 = x + 0.5 * _swiglu(xn, wg_ref, wu_ref, wd_ref)
    x1_ref[...] = x1
    hn = _rms(x1, nm_ref[...]).astype(BF16)
    z = _dot(hn, win_ref[...])
    o2 = 2 * d_lru
    o3 = o2 + q_lora
    o4 = o3 + kv_lora
    xg_ref[...] = z[:, :o2]
    cos = cos_ref[...]
    sin = sin_ref[...]
    tm = x.shape[0]
    lane = lax.broadcasted_iota(jnp.int32, (tm, LANES), 1)
    krr = z[:, o4:o4 + LANES]
    kr = krr * cos[:, :LANES] + pltpu.roll(krr, LANES - QK_ROPE, axis=1) * sin[:, :LANES]
    kr = jnp.where(lane < QK_ROPE, kr, 0.0)
    kr_ref[...] = kr[:, :QK_ROPE]
    ckv = _rms(z[:, o3:o4], kvn_ref[...])
    kv_ref[...] = ckv
    kcat_ref[...] = jnp.concatenate([ckv, kr], axis=1).astype(kcat_ref.dtype)
    cqn = _rms(z[:, o2:o3], qn_ref[...]).astype(BF16)
    q = _dot(cqn, wq_ref[...])
    n_nope = MLA_HEADS * QK_NOPE
    n_rope = MLA_HEADS * QK_ROPE
    qr = q[:, n_nope:n_nope + n_rope] * cos + q[:, n_nope + n_rope:] * sin
    qlat = _dot(q[:, :n_nope].astype(BF16), wuk_ref[...])
    heads_per_vreg = LANES // QK_ROPE
    for h in range(MLA_HEADS):
        blk = qr[:, (h // heads_per_vreg) * LANES:(h // heads_per_vreg + 1) * LANES]
        sh = (h % heads_per_vreg) * QK_ROPE
        if sh:
            blk = pltpu.roll(blk, LANES - sh, axis=1)
        qh = jnp.concatenate([qlat[:, h * kv_lora:(h + 1) * kv_lora], jnp.where(lane < QK_ROPE, blk, 0.0)], axis=1)
        q_ref[h] = (qh * scale).astype(BF16)


def _front(x, cos, sin, n_pos_blocks, w, *, tm, kcat_dtype):
    n, d = x.shape
    d_lru, q_lora, kv_lora = w['d_lru'], w['q_lora'], w['kv_lora']
    nz = w['win'].shape[1]
    scale = float((QK_NOPE + QK_ROPE) ** -0.5)
    row = lambda width: pl.BlockSpec((tm, width), lambda i: (i, 0))
    pos_spec = pl.BlockSpec((tm, cos.shape[1]), lambda i: (i % n_pos_blocks, 0))
    consts = [w['n1'], w['wg1'], w['wu1'], w['wd1'], w['nm'], w['win'], w['qn'], w['wq'], w['kvn'], w['wuk']]
    kern = functools.partial(_front_kernel, d_lru=d_lru, q_lora=q_lora, kv_lora=kv_lora, scale=scale)
    return pl.pallas_call(
        kern,
        grid=(n // tm,),
        in_specs=[row(d)] + [_const_spec(c.shape) for c in consts] + [pos_spec, pos_spec],
        out_specs=[row(d), row(2 * d_lru),
                   pl.BlockSpec((MLA_HEADS, tm, QK_PAD), lambda i: (0, i, 0)),
                   row(QK_PAD), row(kv_lora), row(QK_ROPE)],
        out_shape=[jax.ShapeDtypeStruct((n, d), F32), jax.ShapeDtypeStruct((n, 2 * d_lru), F32),
                   jax.ShapeDtypeStruct((MLA_HEADS, n, QK_PAD), BF16),
                   jax.ShapeDtypeStruct((n, QK_PAD), kcat_dtype),
                   jax.ShapeDtypeStruct((n, kv_lora), F32), jax.ShapeDtypeStruct((n, QK_ROPE), F32)],
        compiler_params=pltpu.CompilerParams(dimension_semantics=("arbitrary",), vmem_limit_bytes=VMEM_LIMIT),
        name="front",
    )(x, *consts, cos, sin)


def _lru_gates(xc, wai_ref, bai_ref, lam_ref, d_lru):
    gates = _dot(xc.astype(BF16), wai_ref[...]) + bai_ref[...]
    r = _sigmoid(gates[:, :d_lru])
    i = _sigmoid(gates[:, d_lru:])
    log_a = (-LRU_C * r) * _softplus(-lam_ref[...])
    a = jnp.exp(log_a)
    u = jnp.sqrt(-jnp.tanh(log_a) * (a * a + 1.0)) * (i * xc)
    return a, u


def _lru_prompt_kernel(xg_ref, conv0_ref, h0_ref, cw_ref, cb_ref, wai_ref, bai_ref, lam_ref, on_ref,
                       y_ref, hlast_ref, convnew_ref, xbuf, a_s, u_s, h_s, *, d_lru, ts):
    s = pl.program_id(1)
    hist = CONV_W - 1

    @pl.when(s == 0)
    def _():
        xbuf[0:SUBLANES, :] = jnp.zeros((SUBLANES, d_lru), F32)
        xbuf[SUBLANES - hist:SUBLANES, :] = conv0_ref[...]
        h_s[...] = jnp.broadcast_to(h0_ref[...], (SUBLANES, d_lru))

    xl = xg_ref[:, :d_lru]
    xbuf[SUBLANES:, :] = xl
    cw = cw_ref[...]
    xc = cb_ref[...] + cw[0:1] * xbuf[SUBLANES - 3:SUBLANES - 3 + ts, :]
    xc = xc + cw[1:2] * xbuf[SUBLANES - 2:SUBLANES - 2 + ts, :]
    xc = xc + cw[2:3] * xbuf[SUBLANES - 1:SUBLANES - 1 + ts, :]
    xc = xc + cw[3:4] * xl
    xbuf[0:SUBLANES, :] = xl[ts - SUBLANES:, :]
    convnew_ref[...] = xl[ts - hist:, :]

    a, u = _lru_gates(xc, wai_ref, bai_ref, lam_ref, d_lru)
    a_s[...] = a
    u_s[...] = u

    row = lax.broadcasted_iota(jnp.int32, (SUBLANES, d_lru), 0)

    def group(g, hb):
        off = pl.multiple_of(g * SUBLANES, SUBLANES)
        ag = a_s[pl.ds(off, SUBLANES), :]
        ug = u_s[pl.ds(off, SUBLANES), :]
        for d in (1, 2, 4):
            keep = row >= d
            a_sh = jnp.where(keep, pltpu.roll(ag, d, axis=0), 1.0)
            u_sh = jnp.where(keep, pltpu.roll(ug, d, axis=0), 0.0)
            ug = ag * u_sh + ug
            ag = ag * a_sh
        hg = ag * hb + ug
        u_s[pl.ds(off, SUBLANES), :] = hg
        return jnp.broadcast_to(hg[SUBLANES - 1:SUBLANES, :], (SUBLANES, d_lru))

    hb = lax.fori_loop(0, ts // SUBLANES, group, h_s[...], unroll=4)
    h_s[...] = hb
    hlast_ref[...] = hb[0:1, :]
    y = _gelu_tanh(xg_ref[:, d_lru:]) * u_s[...]
    y_ref[...] = _rms(y, on_ref[...]).astype(y_ref.dtype)


def _lru_prompt(xg, conv0, h0, w, *, batch, seq, ts):
    d_lru = w['d_lru']
    ns = seq // ts
    hist = CONV_W - 1
    consts = [w['cw'], w['cb'], w['wai'], w['bai'], w['lam'], w['onl']]
    kern = functools.partial(_lru_prompt_kernel, d_lru=d_lru, ts=ts)
    y, hlast, convnew = pl.pallas_call(
        kern,
        grid=(batch, ns),
        in_specs=[pl.BlockSpec((ts, 2 * d_lru), lambda b, s: (b * ns + s, 0)),
                  pl.BlockSpec((None, hist, d_lru), lambda b, s: (b, 0, 0)),
                  pl.BlockSpec((None, 1, d_lru), lambda b, s: (b, 0, 0))]
                 + [pl.BlockSpec(c.shape, lambda b, s, nd=c.ndim: (0,) * nd) for c in consts],
        out_specs=[pl.BlockSpec((ts, d_lru), lambda b, s: (b * ns + s, 0)),
                   pl.BlockSpec((None, 1, d_lru), lambda b, s: (b, 0, 0)),
                   pl.BlockSpec((None, hist, d_lru), lambda b, s: (b, 0, 0))],
        out_shape=[jax.ShapeDtypeStruct((batch * seq, d_lru), BF16),
                   jax.ShapeDtypeStruct((batch, 1, d_lru), F32),
                   jax.ShapeDtypeStruct((batch, hist, d_lru), F32)],
        scratch_shapes=[pltpu.VMEM((ts + SUBLANES, d_lru), F32), pltpu.VMEM((ts, d_lru), F32),
                        pltpu.VMEM((ts, d_lru), F32), pltpu.VMEM((SUBLANES, d_lru), F32)],
        compiler_params=pltpu.CompilerParams(dimension_semantics=("arbitrary", "arbitrary"),
                                             vmem_limit_bytes=VMEM_LIMIT),
        name="lru_prompt",
    )(xg, conv0, h0.reshape(batch, 1, d_lru), *consts)
    return y, hlast.reshape(batch, d_lru), convnew


def _lru_sample_kernel(xg_ref, conv0_ref, h0_ref, cw_ref, cb_ref, wai_ref, bai_ref, lam_ref, on_ref,
                       y_ref, hlast_ref, convnew_ref, *, d_lru, steps):
    hist = CONV_W - 1
    cw = cw_ref[...]
    cb = cb_ref[...]
    xp = [conv0_ref[k] for k in range(hist)] + [xg_ref[t][:, :d_lru] for t in range(steps)]
    xcs = []
    for t in range(steps):
        xc = cb + cw[0:1] * xp[t]
        for k in range(1, CONV_W):
            xc = xc + cw[k:k + 1] * xp[t + k]
        xcs.append(xc)
    nb = xcs[0].shape[0]
    a, u = _lru_gates(jnp.concatenate(xcs, axis=0), wai_ref, bai_ref, lam_ref, d_lru)
    h = h0_ref[...]
    for t in range(steps):
        h = a[t * nb:(t + 1) * nb] * h + u[t * nb:(t + 1) * nb]
        y = _gelu_tanh(xg_ref[t][:, d_lru:]) * h
        y_ref[t] = _rms(y, on_ref[...]).astype(y_ref.dtype)
    hlast_ref[...] = h
    for k in range(hist):
        convnew_ref[k] = xp[steps + k]


def _lru_sample(xg, conv0_t, h0, w, *, steps, nb):
    d_lru = w['d_lru']
    hist = CONV_W - 1
    kern = functools.partial(_lru_sample_kernel, d_lru=d_lru, steps=steps)
    return pl.pallas_call(
        kern,
        out_shape=[jax.ShapeDtypeStruct((steps, nb, d_lru), BF16),
                   jax.ShapeDtypeStruct((nb, d_lru), F32),
                   jax.ShapeDtypeStruct((hist, nb, d_lru), F32)],
        compiler_params=pltpu.CompilerParams(vmem_limit_bytes=VMEM_LIMIT),
        name="lru_sample",
    )(xg.reshape(steps, nb, 2 * d_lru), conv0_t, h0, w['cw'], w['cb'], w['wai'], w['bai'], w['lam'], w['onl'])


def _softmax_step(s, v, m_s, l_s, acc_s):
    m_prev = m_s[...]
    m_next = jnp.maximum(m_prev, jnp.max(s, axis=1, keepdims=True))
    p = jnp.exp(s - m_next[:, 0:1])
    alpha = jnp.exp(m_prev - m_next)
    l_s[...] = alpha * l_s[...] + jnp.sum(p, axis=1, keepdims=True)
    m_s[...] = m_next
    acc_s[...] = alpha * acc_s[...] + _dot(p.astype(BF16), v)


def _scores(q, k):
    return lax.dot_general(q, k, (((1,), (1,)), ((), ())), preferred_element_type=F32)


def _mla_out(acc_s, l_s, wuv_ref, on_ref, rows):
    o = (acc_s[...] / l_s[...]).astype(BF16)
    y = _dot(o[0:rows], wuv_ref[0])
    for h in range(1, MLA_HEADS):
        y = y + _dot(o[h * rows:(h + 1) * rows], wuv_ref[h])
    return _rms(y, on_ref[...])


def _attn_prompt_kernel(q_ref, k_ref, wuv_ref, on_ref, y_ref, m_s, l_s, acc_s, *, tq, kv_lora):
    qi = pl.program_id(1)
    rows = MLA_HEADS * tq
    q = q_ref[...].reshape(rows, QK_PAD)
    m_s[...] = jnp.full(m_s.shape, -jnp.inf, F32)
    l_s[...] = jnp.zeros(l_s.shape, F32)
    acc_s[...] = jnp.zeros(acc_s.shape, F32)

    def past_block(j, carry):
        k = k_ref[pl.ds(pl.multiple_of(j * tq, tq), tq), :]
        _softmax_step(_scores(q, k), k[:, :kv_lora], m_s, l_s, acc_s)
        return carry

    lax.fori_loop(0, qi, past_block, 0)
    k = k_ref[pl.ds(pl.multiple_of(qi * tq, tq), tq), :]
    r = lax.broadcasted_iota(jnp.int32, (rows, tq), 0) & (tq - 1)
    c = lax.broadcasted_iota(jnp.int32, (rows, tq), 1)
    s = jnp.where(c <= r, _scores(q, k), -jnp.inf)
    _softmax_step(s, k[:, :kv_lora], m_s, l_s, acc_s)
    y_ref[...] = _mla_out(acc_s, l_s, wuv_ref, on_ref, tq).astype(y_ref.dtype)


def _attn_prompt(q, kcat, w, *, batch, seq, tq):
    kv_lora = w['kv_lora']
    d_mla = MLA_HEADS * V_DIM
    rows = MLA_HEADS * tq
    assert tq & (tq - 1) == 0 and seq % tq == 0
    kern = functools.partial(_attn_prompt_kernel, tq=tq, kv_lora=kv_lora)
    return pl.pallas_call(
        kern,
        grid=(batch, seq // tq),
        in_specs=[pl.BlockSpec((MLA_HEADS, None, tq, QK_PAD), lambda b, i: (0, b, i, 0)),
                  pl.BlockSpec((None, seq, QK_PAD), lambda b, i: (b, 0, 0)),
                  pl.BlockSpec(w['wuv'].shape, lambda b, i: (0, 0, 0)),
                  pl.BlockSpec(w['onm'].shape, lambda b, i: (0, 0))],
        out_specs=pl.BlockSpec((None, tq, d_mla), lambda b, i: (b, i, 0)),
        out_shape=jax.ShapeDtypeStruct((batch, seq, d_mla), BF16),
        scratch_shapes=[pltpu.VMEM((rows, LANES), F32), pltpu.VMEM((rows, LANES), F32),
                        pltpu.VMEM((rows, kv_lora), F32)],
        compiler_params=pltpu.CompilerParams(dimension_semantics=("arbitrary", "arbitrary"),
                                             vmem_limit_bytes=VMEM_LIMIT),
        name="attn_prompt",
    )(q.reshape(MLA_HEADS, batch, seq, QK_PAD), kcat.reshape(batch, seq, QK_PAD), w['wuv'], w['onm'])


def _attn_sample_kernel(pt_ref, q_ref, knew_ref, wuv_ref, on_ref, ckv_hbm, kr_hbm, y_ref,
                        ckv_buf, kr_buf, sems, m_s, l_s, acc_s, *, steps, n_pages, chunk_pages, kv_lora):
    b = pl.program_id(0)
    nb = pl.num_programs(0)
    n_chunks = n_pages // chunk_pages
    chunk_keys = chunk_pages * PAGE_SIZE
    rows = MLA_HEADS * steps

    def page_copies(bb, j, slot, p):
        page = pt_ref[bb, j * chunk_pages + p]
        return (pltpu.make_async_copy(ckv_hbm.at[page], ckv_buf.at[slot, p], sems.at[0, slot]),
                pltpu.make_async_copy(kr_hbm.at[page], kr_buf.at[slot, p], sems.at[1, slot]))

    def start_chunk(bb, j, slot):
        for p in range(chunk_pages):
            for cp in page_copies(bb, j, slot, p):
                cp.start()

    def wait_chunk(bb, j, slot):
        for p in range(chunk_pages):
            for cp in page_copies(bb, j, slot, p):
                cp.wait()

    @pl.when(b == 0)
    def _():
        start_chunk(b, 0, 0)

    q = q_ref[...]
    m_s[...] = jnp.full(m_s.shape, -jnp.inf, F32)
    l_s[...] = jnp.zeros(l_s.shape, F32)
    acc_s[...] = jnp.zeros(acc_s.shape, F32)

    for j in range(n_chunks):
        slot = j % 2
        if j + 1 < n_chunks:
            start_chunk(b, j + 1, 1 - slot)
        else:
            @pl.when(b + 1 < nb)
            def _():
                start_chunk(b + 1, 0, 1 - slot)
        wait_chunk(b, j, slot)
        ckv = ckv_buf[slot].reshape(chunk_keys, kv_lora).astype(BF16)
        kr = kr_buf[slot].reshape(chunk_keys, QK_ROPE)
        kr = jnp.concatenate([kr, jnp.zeros((chunk_keys, LANES - QK_ROPE), F32)], axis=1).astype(BF16)
        k = jnp.concatenate([ckv, kr], axis=1)
        _softmax_step(_scores(q, k), ckv, m_s, l_s, acc_s)

    knew = jnp.concatenate([knew_ref[...], jnp.zeros((LANES - steps, QK_PAD), F32)], axis=0).astype(BF16)
    r = lax.broadcasted_iota(jnp.int32, (rows, LANES), 0) & (steps - 1)
    c = lax.broadcasted_iota(jnp.int32, (rows, LANES), 1)
    s = jnp.where(c <= r, _scores(q, knew), -jnp.inf)
    _softmax_step(s, knew[:, :kv_lora], m_s, l_s, acc_s)
    y_ref[...] = _mla_out(acc_s, l_s, wuv_ref, on_ref, steps).astype(y_ref.dtype)


def _attn_sample(page_table, q, knew, cache_c, cache_r, w, *, nb, steps, chunk_pages):
    kv_lora = w['kv_lora']
    d_mla = MLA_HEADS * V_DIM
    rows = MLA_HEADS * steps
    n_pages = page_table.shape[1]
    assert n_pages % (2 * chunk_pages) == 0 and steps & (steps - 1) == 0 and steps <= LANES
    kern = functools.partial(_attn_sample_kernel, steps=steps, n_pages=n_pages, chunk_pages=chunk_pages,
                             kv_lora=kv_lora)
    grid_spec = pltpu.PrefetchScalarGridSpec(
        num_scalar_prefetch=1,
        grid=(nb,),
        in_specs=[pl.BlockSpec((None, rows, QK_PAD), lambda b, pt: (b, 0, 0)),
                  pl.BlockSpec((None, steps, QK_PAD), lambda b, pt: (b, 0, 0)),
                  pl.BlockSpec(w['wuv'].shape, lambda b, pt: (0, 0, 0)),
                  pl.BlockSpec(w['onm'].shape, lambda b, pt: (0, 0)),
                  pl.BlockSpec(memory_space=pl.ANY),
                  pl.BlockSpec(memory_space=pl.ANY)],
        out_specs=pl.BlockSpec((None, steps, d_mla), lambda b, pt: (b, 0, 0)),
        scratch_shapes=[pltpu.VMEM((2, chunk_pages, PAGE_SIZE, kv_lora), F32),
                        pltpu.VMEM((2, chunk_pages, PAGE_SIZE, QK_ROPE), F32),
                        pltpu.SemaphoreType.DMA((2, 2)),
                        pltpu.VMEM((rows, LANES), F32), pltpu.VMEM((rows, LANES), F32),
                        pltpu.VMEM((rows, kv_lora), F32)],
    )
    return pl.pallas_call(
        kern,
        grid_spec=grid_spec,
        out_shape=jax.ShapeDtypeStruct((nb, steps, d_mla), F32),
        compiler_params=pltpu.CompilerParams(dimension_semantics=("arbitrary",), vmem_limit_bytes=VMEM_LIMIT),
        name="attn_sample",
    )(page_table, q, knew, w['wuv'], w['onm'], cache_c, cache_r)


def _back_kernel(x1_ref, yl_ref, ym_ref, wol_ref, wom_ref, n2_ref, wg_ref, wu_ref, wd_ref, nf_ref, y_ref):
    x2 = x1_ref[...] + (_dot(yl_ref[...], wol_ref[...]) + _dot(ym_ref[...], wom_ref[...]))
    xn = _rms(x2, n2_ref[...]).astype(BF16)
    x3 = x2 + 0.5 * _swiglu(xn, wg_ref, wu_ref, wd_ref)
    y_ref[...] = _rms(x3, nf_ref[...])


def _back(x1, yl, ym, w, *, tm):
    n, d = x1.shape
    row = lambda width: pl.BlockSpec((tm, width), lambda i: (i, 0))
    consts = [w['wol'], w['wom'], w['n2'], w['wg2'], w['wu2'], w['wd2'], w['nf']]
    return pl.pallas_call(
        _back_kernel,
        grid=(n // tm,),
        in_specs=[row(d), row(yl.shape[1]), row(ym.shape[1])] + [_const_spec(c.shape) for c in consts],
        out_specs=row(d),
        out_shape=jax.ShapeDtypeStruct((n, d), F32),
        compiler_params=pltpu.CompilerParams(dimension_semantics=("arbitrary",), vmem_limit_bytes=VMEM_LIMIT),
        name="back",
    )(x1, yl, ym, *consts)


def _rot_cols(wr):
    half = wr.shape[-1] // 2
    return jnp.concatenate([-wr[..., half:], wr[..., :half]], axis=-1)


def _block_diag(wb):
    n, di, do = wb.shape
    eye = jnp.eye(n, dtype=wb.dtype)
    return (wb[:, :, None, :] * eye[:, None, :, None]).reshape(n * di, n * do)


def _prep_weights(l, ffn1_norm, ffn1_w_gate, ffn1_w_up, ffn1_w_down, mix_norm, w_in, conv_w, conv_b,
                  lru_w_a, lru_b_a, lru_w_i, lru_b_i, lru_lambda, q_norm, w_q_up, kv_norm, w_uk, w_uv,
                  out_norm_lru, out_norm_mla, w_out, ffn2_norm, ffn2_w_gate, ffn2_w_up, ffn2_w_down,
                  final_norm):
    d_lru = conv_w.shape[-1]
    q_lora = q_norm.shape[-1]
    kv_lora = kv_norm.shape[-1]
    d_model = w_in.shape[1]
    o4 = 2 * d_lru + q_lora + kv_lora
    win = w_in[l]
    wkr = win[:, o4:]
    win_ext = jnp.concatenate(
        [win[:, :o4], wkr, _rot_cols(wkr), jnp.zeros((d_model, LANES - 2 * QK_ROPE), F32)], axis=1)
    wq = w_q_up[l].reshape(q_lora, MLA_HEADS, QK_NOPE + QK_ROPE)
    wq_rope = wq[:, :, QK_NOPE:]
    wq_ext = jnp.concatenate([wq[:, :, :QK_NOPE].reshape(q_lora, -1), wq_rope.reshape(q_lora, -1),
                              _rot_cols(wq_rope).reshape(q_lora, -1)], axis=1)
    wuk_bd = _block_diag(jnp.transpose(w_uk[l], (1, 2, 0)))
    eye = jnp.eye(MLA_HEADS, dtype=F32)
    wuv_ext = (jnp.transpose(w_uv[l], (1, 0, 2))[:, :, None, :] * eye[:, None, :, None]).reshape(
        MLA_HEADS, kv_lora, MLA_HEADS * V_DIM)
    row = lambda v: v.reshape(1, -1)
    return {
        'd_lru': d_lru, 'q_lora': q_lora, 'kv_lora': kv_lora,
        'n1': row(ffn1_norm[l]), 'wg1': ffn1_w_gate[l].astype(BF16), 'wu1': ffn1_w_up[l].astype(BF16),
        'wd1': ffn1_w_down[l].astype(BF16), 'nm': row(mix_norm[l]), 'win': win_ext.astype(BF16),
        'qn': row(q_norm[l]), 'wq': wq_ext.astype(BF16), 'kvn': row(kv_norm[l]), 'wuk': wuk_bd.astype(BF16),
        'cw': conv_w[l], 'cb': row(conv_b[l]),
        'wai': jnp.concatenate([_block_diag(lru_w_a[l]), _block_diag(lru_w_i[l])], axis=1).astype(BF16),
        'bai': row(jnp.concatenate([lru_b_a[l], lru_b_i[l]])), 'lam': row(lru_lambda[l]),
        'onl': row(out_norm_lru[l]), 'wuv': wuv_ext.astype(BF16), 'onm': row(out_norm_mla[l]),
        'wol': w_out[l, :d_lru].astype(BF16), 'wom': w_out[l, d_lru:].astype(BF16),
        'n2': row(ffn2_norm[l]), 'wg2': ffn2_w_gate[l].astype(BF16), 'wu2': ffn2_w_up[l].astype(BF16),
        'wd2': ffn2_w_down[l].astype(BF16), 'nf': row(final_norm),
    }


def _rope_tables(pos):
    half = QK_ROPE // 2
    freqs = ROPE_BASE ** (-jnp.arange(half, dtype=F32) / half)
    ang = pos.astype(F32)[:, None] * freqs[None, :]
    return jnp.tile(jnp.cos(ang), (1, 2 * MLA_HEADS)), jnp.tile(jnp.sin(ang), (1, 2 * MLA_HEADS))


FRONT_ROWS = 256
BACK_ROWS = 256
LRU_ROWS = 512
ATTN_ROWS = 256
SAMPLE_CHUNK_PAGES = 16


def kernel(x_prompt, x_sample, cache_kv_latent, cache_k_rope, state_lru_h, state_conv, page_table, ffn1_norm, ffn1_w_gate, ffn1_w_up, ffn1_w_down, mix_norm, w_in, conv_w, conv_b, lru_w_a, lru_b_a, lru_w_i, lru_b_i, lru_lambda, q_norm, w_q_up, kv_norm, w_uk, w_uv, out_norm_lru, out_norm_mla, w_out, ffn2_norm, ffn2_w_gate, ffn2_w_up, ffn2_w_down, final_norm):
    B, S, D = x_prompt.shape
    DB, DS, _ = x_sample.shape
    depth = ffn1_norm.shape[0]
    assert depth == 1 and S >= CONV_W - 1 and DS >= CONV_W - 1
    past_len = page_table.shape[1] * PAGE_SIZE
    l = 0
    w = _prep_weights(l, ffn1_norm, ffn1_w_gate, ffn1_w_up, ffn1_w_down, mix_norm, w_in, conv_w, conv_b,
                      lru_w_a, lru_b_a, lru_w_i, lru_b_i, lru_lambda, q_norm, w_q_up, kv_norm, w_uk, w_uv,
                      out_norm_lru, out_norm_mla, w_out, ffn2_norm, ffn2_w_gate, ffn2_w_up, ffn2_w_down,
                      final_norm)
    d_lru, kv_lora = w['d_lru'], w['kv_lora']
    hist = CONV_W - 1

    cos_p, sin_p = _rope_tables(jnp.arange(S, dtype=jnp.int32))
    x1, xg, q, kcat, kv_p, kr_p = _front(x_prompt.reshape(B * S, D), cos_p, sin_p, S // FRONT_ROWS, w,
                                         tm=FRONT_ROWS, kcat_dtype=BF16)
    yl, h_p, conv_p = _lru_prompt(xg, jnp.zeros((B, hist, d_lru), F32), jnp.zeros((B, d_lru), F32), w,
                                  batch=B, seq=S, ts=LRU_ROWS)
    ym = _attn_prompt(q, kcat, w, batch=B, seq=S, tq=ATTN_ROWS)
    y_p = _back(x1, yl, ym.reshape(B * S, -1), w, tm=BACK_ROWS).reshape(B, S, D)

    ns = DS * DB
    cos_s, sin_s = _rope_tables(jnp.repeat(past_len + jnp.arange(DS, dtype=jnp.int32), DB))
    xs = jnp.transpose(x_sample, (1, 0, 2)).reshape(ns, D)
    x1s, xgs, qs, kcats, kv_s, kr_s = _front(xs, cos_s, sin_s, ns // FRONT_ROWS, w, tm=FRONT_ROWS, kcat_dtype=F32)
    yls, h_s, conv_s = _lru_sample(xgs, jnp.transpose(state_conv[l], (1, 0, 2)), state_lru_h[l], w,
                                   steps=DS, nb=DB)
    q_b = jnp.transpose(qs.reshape(MLA_HEADS, DS, DB, QK_PAD), (2, 0, 1, 3)).reshape(DB, MLA_HEADS * DS, QK_PAD)
    knew_b = jnp.transpose(kcats.reshape(DS, DB, QK_PAD), (1, 0, 2))
    yms = _attn_sample(page_table, q_b, knew_b, cache_kv_latent[l], cache_k_rope[l], w,
                       nb=DB, steps=DS, chunk_pages=SAMPLE_CHUNK_PAGES)
    yms_t = jnp.transpose(yms, (1, 0, 2)).reshape(ns, -1).astype(BF16)
    y_s = _back(x1s, yls.reshape(ns, d_lru), yms_t, w, tm=BACK_ROWS)
    to_b = lambda a: jnp.transpose(a.reshape(DS, DB, -1), (1, 0, 2))

    return (y_p, to_b(y_s),
            kv_p.reshape(1, B, S, kv_lora), kr_p.reshape(1, B, S, QK_ROPE),
            h_p[None], conv_p[None],
            to_b(kv_s)[None], to_b(kr_s)[None],
            h_s[None], jnp.transpose(conv_s, (1, 0, 2))[None])
```

```python
import functools

import numpy as np
import jax
import jax.numpy as jnp
from jax import lax
from jax.experimental import pallas as pl
from jax.experimental.pallas import tpu as pltpu

LRU_BLOCKS = 8
CONV_W = 4
LRU_C = 8.0
MLA_HEADS = 8
QK_NOPE = 64
QK_ROPE = 32
V_DIM = 64
ROPE_BASE = 10000.0
EPS = 1e-6
PAGE_SIZE = 128

LANES = 128
SUBLANES = 8
QK_PAD = 2 * LANES
VMEM_LIMIT = 56 * 1024 * 1024

F32 = jnp.float32
BF16 = jnp.bfloat16


def _rms(x, g):
    return x * lax.rsqrt(jnp.mean(x * x, axis=-1, keepdims=True) + EPS) * g


def _sigmoid(x):
    return 1.0 / (1.0 + jnp.exp(-x))


def _gelu_tanh(x):
    c = np.sqrt(2.0 / np.pi).astype(np.float32)
    return 0.5 * x * (1.0 + jnp.tanh(c * (x + 0.044715 * (x * x * x))))


def _softplus(x):
    return jnp.maximum(x, 0.0) + jnp.log1p(jnp.exp(-jnp.abs(x)))


def _dot(a, b):
    return jnp.dot(a, b, preferred_element_type=F32)


def _swiglu(xn, wg_ref, wu_ref, wd_ref):
    g = _dot(xn, wg_ref[...])
    u = _dot(xn, wu_ref[...])
    h = (g * _sigmoid(g)) * u
    return _dot(h.astype(BF16), wd_ref[...])


def _const_spec(shape):
    nd = len(shape)
    return pl.BlockSpec(shape, lambda *_: (0,) * nd, pipeline_mode=pl.Buffered(1))


def _front_kernel(x_ref, n1_ref, wg_ref, wu_ref, wd_ref, nm_ref, win_ref, qn_ref, wq_ref, kvn_ref,
                  wuk_ref, cos_ref, sin_ref,
                  x1_ref, xg_ref, qt_ref, kcat_ref, vt_ref, kv_ref, kr_ref, *, d_lru, q_lora, kv_lora, scale):
    x = x_ref[...]
    xn = _rms(x, n1_ref[...]).astype(BF16)
    x1 = x + 0.5 * _swiglu(xn, wg_ref, wu_ref, wd_ref)
    x1_ref[...] = x1
    hn = _rms(x1, nm_ref[...]).astype(BF16)
    z = _dot(hn, win_ref[...])
    o2 = 2 * d_lru
    o3 = o2 + q_lora
    o4 = o3 + kv_lora
    xg_ref[...] = z[:, :o2]
    cos = cos_ref[...]
    sin = sin_ref[...]
    tm = x.shape[0]
    lane = lax.broadcasted_iota(jnp.int32, (tm, LANES), 1)
    krr = z[:, o4:o4 + LANES]
    kr = krr * cos[:, :LANES] + pltpu.roll(krr, LANES - QK_ROPE, axis=1) * sin[:, :LANES]
    kr = jnp.where(lane < QK_ROPE, kr, 0.0)
    kr_ref[...] = kr[:, :QK_ROPE]
    ckv = _rms(z[:, o3:o4], kvn_ref[...])
    kv_ref[...] = ckv
    vt_ref[...] = ckv.T.astype(BF16)
    kcat_ref[...] = jnp.concatenate([ckv, kr], axis=1).astype(kcat_ref.dtype)
    cqn = _rms(z[:, o2:o3], qn_ref[...]).astype(BF16)
    q = _dot(cqn, wq_ref[...])
    n_nope = MLA_HEADS * QK_NOPE
    n_rope = MLA_HEADS * QK_ROPE
    qr = q[:, n_nope:n_nope + n_rope] * cos + q[:, n_nope + n_rope:] * sin
    qlat = _dot(q[:, :n_nope].astype(BF16), wuk_ref[...])
    heads_per_vreg = LANES // QK_ROPE
    for h in range(MLA_HEADS):
        blk = qr[:, (h // heads_per_vreg) * LANES:(h // heads_per_vreg + 1) * LANES]
        sh = (h % heads_per_vreg) * QK_ROPE
        if sh:
            blk = pltpu.roll(blk, LANES - sh, axis=1)
        qh = jnp.concatenate([qlat[:, h * kv_lora:(h + 1) * kv_lora], jnp.where(lane < QK_ROPE, blk, 0.0)], axis=1)
        qt_ref[:, h * tm:(h + 1) * tm] = (qh * scale).T.astype(BF16)


def _front(x, cos, sin, n_pos_blocks, w, *, tm, kcat_dtype):
    n, d = x.shape
    d_lru, q_lora, kv_lora = w['d_lru'], w['q_lora'], w['kv_lora']
    nz = w['win'].shape[1]
    scale = float((QK_NOPE + QK_ROPE) ** -0.5)
    row = lambda width: pl.BlockSpec((tm, width), lambda i: (i, 0))
    pos_spec = pl.BlockSpec((tm, cos.shape[1]), lambda i: (i % n_pos_blocks, 0))
    consts = [w['n1'], w['wg1'], w['wu1'], w['wd1'], w['nm'], w['win'], w['qn'], w['wq'], w['kvn'], w['wuk']]
    kern = functools.partial(_front_kernel, d_lru=d_lru, q_lora=q_lora, kv_lora=kv_lora, scale=scale)
    return pl.pallas_call(
        kern,
        grid=(n // tm,),
        in_specs=[row(d)] + [_const_spec(c.shape) for c in consts] + [pos_spec, pos_spec],
        out_specs=[row(d), row(2 * d_lru),
                   pl.BlockSpec((None, QK_PAD, MLA_HEADS * tm), lambda i: (i, 0, 0)),
                   row(QK_PAD),
                   pl.BlockSpec((kv_lora, tm), lambda i: (0, i)),
                   row(kv_lora), row(QK_ROPE)],
        out_shape=[jax.ShapeDtypeStruct((n, d), F32), jax.ShapeDtypeStruct((n, 2 * d_lru), F32),
                   jax.ShapeDtypeStruct((n // tm, QK_PAD, MLA_HEADS * tm), BF16),
                   jax.ShapeDtypeStruct((n, QK_PAD), kcat_dtype),
                   jax.ShapeDtypeStruct((kv_lora, n), BF16),
                   jax.ShapeDtypeStruct((n, kv_lora), F32), jax.ShapeDtypeStruct((n, QK_ROPE), F32)],
        compiler_params=pltpu.CompilerParams(dimension_semantics=("arbitrary",), vmem_limit_bytes=VMEM_LIMIT),
        name="front",
    )(x, *consts, cos, sin)


def _lru_gates(xc, wai_ref, bai_ref, lam_ref, d_lru):
    gates = _dot(xc.astype(BF16), wai_ref[...]) + bai_ref[...]
    r = _sigmoid(gates[:, :d_lru])
    i = _sigmoid(gates[:, d_lru:])
    log_a = (-LRU_C * r) * _softplus(-lam_ref[...])
    a = jnp.exp(log_a)
    u = jnp.sqrt(-jnp.tanh(log_a) * (a * a + 1.0)) * (i * xc)
    return a, u


def _lru_prompt_kernel(xg_ref, conv0_ref, h0_ref, cw_ref, cb_ref, wai_ref, bai_ref, lam_ref, on_ref,
                       y_ref, hlast_ref, convnew_ref, xbuf, a_s, u_s, h_s, *, d_lru, ts):
    s = pl.program_id(1)
    hist = CONV_W - 1

    @pl.when(s == 0)
    def _():
        xbuf[0:SUBLANES, :] = jnp.zeros((SUBLANES, d_lru), F32)
        xbuf[SUBLANES - hist:SUBLANES, :] = conv0_ref[...]
        h_s[...] = jnp.broadcast_to(h0_ref[...], (SUBLANES, d_lru))

    xl = xg_ref[:, :d_lru]
    xbuf[SUBLANES:, :] = xl
    cw = cw_ref[...]
    xc = cb_ref[...] + cw[0:1] * xbuf[SUBLANES - 3:SUBLANES - 3 + ts, :]
    xc = xc + cw[1:2] * xbuf[SUBLANES - 2:SUBLANES - 2 + ts, :]
    xc = xc + cw[2:3] * xbuf[SUBLANES - 1:SUBLANES - 1 + ts, :]
    xc = xc + cw[3:4] * xl
    xbuf[0:SUBLANES, :] = xl[ts - SUBLANES:, :]
    convnew_ref[...] = xl[ts - hist:, :]

    a, u = _lru_gates(xc, wai_ref, bai_ref, lam_ref, d_lru)
    a_s[...] = a
    u_s[...] = u

    row = lax.broadcasted_iota(jnp.int32, (SUBLANES, d_lru), 0)

    def group(g, hb):
        off = pl.multiple_of(g * SUBLANES, SUBLANES)
        ag = a_s[pl.ds(off, SUBLANES), :]
        ug = u_s[pl.ds(off, SUBLANES), :]
        for d in (1, 2, 4):
            keep = row >= d
            a_sh = jnp.where(keep, pltpu.roll(ag, d, axis=0), 1.0)
            u_sh = jnp.where(keep, pltpu.roll(ug, d, axis=0), 0.0)
            ug = ag * u_sh + ug
            ag = ag * a_sh
        hg = ag * hb + ug
        u_s[pl.ds(off, SUBLANES), :] = hg
        return jnp.broadcast_to(hg[SUBLANES - 1:SUBLANES, :], (SUBLANES, d_lru))

    hb = lax.fori_loop(0, ts // SUBLANES, group, h_s[...], unroll=4)
    h_s[...] = hb
    hlast_ref[...] = hb[0:1, :]
    y = _gelu_tanh(xg_ref[:, d_lru:]) * u_s[...]
    y_ref[...] = _rms(y, on_ref[...]).astype(y_ref.dtype)


def _lru_prompt(xg, conv0, h0, w, *, batch, seq, ts):
    d_lru = w['d_lru']
    ns = seq // ts
    hist = CONV_W - 1
    consts = [w['cw'], w['cb'], w['wai'], w['bai'], w['lam'], w['onl']]
    kern = functools.partial(_lru_prompt_kernel, d_lru=d_lru, ts=ts)
    y, hlast, convnew = pl.pallas_call(
        kern,
        grid=(batch, ns),
        in_specs=[pl.BlockSpec((ts, 2 * d_lru), lambda b, s: (b * ns + s, 0)),
                  pl.BlockSpec((None, hist, d_lru), lambda b, s: (b, 0, 0)),
                  pl.BlockSpec((None, 1, d_lru), lambda b, s: (b, 0, 0))]
                 + [pl.BlockSpec(c.shape, lambda b, s, nd=c.ndim: (0,) * nd) for c in consts],
        out_specs=[pl.BlockSpec((ts, d_lru), lambda b, s: (b * ns + s, 0)),
                   pl.BlockSpec((None, 1, d_lru), lambda b, s: (b, 0, 0)),
                   pl.BlockSpec((None, hist, d_lru), lambda b, s: (b, 0, 0))],
        out_shape=[jax.ShapeDtypeStruct((batch * seq, d_lru), BF16),
                   jax.ShapeDtypeStruct((batch, 1, d_lru), F32),
                   jax.ShapeDtypeStruct((batch, hist, d_lru), F32)],
        scratch_shapes=[pltpu.VMEM((ts + SUBLANES, d_lru), F32), pltpu.VMEM((ts, d_lru), F32),
                        pltpu.VMEM((ts, d_lru), F32), pltpu.VMEM((SUBLANES, d_lru), F32)],
        compiler_params=pltpu.CompilerParams(dimension_semantics=("arbitrary", "arbitrary"),
                                             vmem_limit_bytes=VMEM_LIMIT),
        name="lru_prompt",
    )(xg, conv0, h0.reshape(batch, 1, d_lru), *consts)
    return y, hlast.reshape(batch, d_lru), convnew


def _lru_sample_kernel(xg_ref, conv0_ref, h0_ref, cw_ref, cb_ref, wai_ref, bai_ref, lam_ref, on_ref,
                       y_ref, hlast_ref, convnew_ref, *, d_lru, steps):
    hist = CONV_W - 1
    cw = cw_ref[...]
    cb = cb_ref[...]
    xp = [conv0_ref[k] for k in range(hist)] + [xg_ref[t][:, :d_lru] for t in range(steps)]
    xcs = []
    for t in range(steps):
        xc = cb + cw[0:1] * xp[t]
        for k in range(1, CONV_W):
            xc = xc + cw[k:k + 1] * xp[t + k]
        xcs.append(xc)
    nb = xcs[0].shape[0]
    a, u = _lru_gates(jnp.concatenate(xcs, axis=0), wai_ref, bai_ref, lam_ref, d_lru)
    h = h0_ref[...]
    for t in range(steps):
        h = a[t * nb:(t + 1) * nb] * h + u[t * nb:(t + 1) * nb]
        y = _gelu_tanh(xg_ref[t][:, d_lru:]) * h
        y_ref[t] = _rms(y, on_ref[...]).astype(y_ref.dtype)
    hlast_ref[...] = h
    for k in range(hist):
        convnew_ref[k] = xp[steps + k]


def _lru_sample(xg, conv0_t, h0, w, *, steps, nb):
    d_lru = w['d_lru']
    hist = CONV_W - 1
    kern = functools.partial(_lru_sample_kernel, d_lru=d_lru, steps=steps)
    return pl.pallas_call(
        kern,
        out_shape=[jax.ShapeDtypeStruct((steps, nb, d_lru), BF16),
                   jax.ShapeDtypeStruct((nb, d_lru), F32),
                   jax.ShapeDtypeStruct((hist, nb, d_lru), F32)],
        compiler_params=pltpu.CompilerParams(vmem_limit_bytes=VMEM_LIMIT),
        name="lru_sample",
    )(xg.reshape(steps, nb, 2 * d_lru), conv0_t, h0, w['cw'], w['cb'], w['wai'], w['bai'], w['lam'], w['onl'])


def _attn_prompt_kernel(qt_ref, k_ref, vt_ref, wuvt_ref, on_ref, y_ref, m_s, l_s, acc_s, *, tq):
    qi = pl.program_id(1)
    cols = MLA_HEADS * tq
    qt = qt_ref[...]
    m_s[...] = jnp.full(m_s.shape, -jnp.inf, F32)
    l_s[...] = jnp.zeros(l_s.shape, F32)
    acc_s[...] = jnp.zeros(acc_s.shape, F32)

    def block(j, masked):
        off = pl.multiple_of(j * tq, tq)
        s = _dot(k_ref[pl.ds(off, tq), :], qt)
        if masked:
            key = lax.broadcasted_iota(jnp.int32, (tq, cols), 0)
            qry = lax.broadcasted_iota(jnp.int32, (tq, cols), 1) & (tq - 1)
            s = jnp.where(key <= qry, s, -jnp.inf)
        m_prev = m_s[...]
        m_next = jnp.maximum(m_prev, jnp.max(s, axis=0, keepdims=True))
        p = jnp.exp(s - m_next)
        alpha = jnp.exp(m_prev - m_next)
        l_s[...] = alpha * l_s[...] + jnp.sum(p, axis=0, keepdims=True)
        m_s[...] = m_next
        acc_s[...] = alpha * acc_s[...] + _dot(vt_ref[:, pl.ds(off, tq)], p.astype(BF16))

    def past_block(j, carry):
        block(j, False)
        return carry

    lax.fori_loop(0, qi, past_block, 0)
    block(qi, True)
    ot = (acc_s[...] / l_s[...]).astype(BF16)
    yt = _dot(wuvt_ref[0], ot[:, 0:tq])
    for h in range(1, MLA_HEADS):
        yt = yt + _dot(wuvt_ref[h], ot[:, h * tq:(h + 1) * tq])
    y_ref[...] = _rms(yt.T, on_ref[...]).astype(y_ref.dtype)


def _attn_prompt(qt, kcat, vt, w, *, batch, seq, tq):
    kv_lora = w['kv_lora']
    d_mla = MLA_HEADS * V_DIM
    cols = MLA_HEADS * tq
    nq = seq // tq
    assert tq & (tq - 1) == 0 and seq % tq == 0 and qt.shape == (batch * nq, QK_PAD, cols)
    kern = functools.partial(_attn_prompt_kernel, tq=tq)
    return pl.pallas_call(
        kern,
        grid=(batch, nq),
        in_specs=[pl.BlockSpec((None, QK_PAD, cols), lambda b, i: (b * nq + i, 0, 0)),
                  pl.BlockSpec((None, seq, QK_PAD), lambda b, i: (b, 0, 0)),
                  pl.BlockSpec((kv_lora, seq), lambda b, i: (0, b)),
                  pl.BlockSpec(w['wuvt'].shape, lambda b, i: (0, 0, 0)),
                  pl.BlockSpec(w['onm'].shape, lambda b, i: (0, 0))],
        out_specs=pl.BlockSpec((None, tq, d_mla), lambda b, i: (b, i, 0)),
        out_shape=jax.ShapeDtypeStruct((batch, seq, d_mla), BF16),
        scratch_shapes=[pltpu.VMEM((1, cols), F32), pltpu.VMEM((1, cols), F32),
                        pltpu.VMEM((kv_lora, cols), F32)],
        compiler_params=pltpu.CompilerParams(dimension_semantics=("arbitrary", "arbitrary"),
                                             vmem_limit_bytes=VMEM_LIMIT),
        name="attn_prompt",
    )(qt, kcat.reshape(batch, seq, QK_PAD), vt, w['wuvt'], w['onm'])


def _scores(q, k):
    return lax.dot_general(q, k, (((1,), (1,)), ((), ())), preferred_element_type=F32)


def _softmax_part(s, v):
    m = jnp.max(s, axis=1, keepdims=True)
    p = jnp.exp(s - m)
    return m, jnp.sum(p, axis=1, keepdims=True), _dot(p.astype(BF16), v)


def _attn_sample_kernel(pt_ref, q_ref, knew_ref, wuv_ref, on_ref, ckv_hbm, krt_hbm, y_ref,
                        ckv_buf, krt_buf, sems, *, layer, steps, n_pages, n_chains, kv_lora):
    b = pl.program_id(0)
    nb = pl.num_programs(0)
    slot = b % 2
    rows = MLA_HEADS * steps
    chain_pages = n_pages // n_chains
    chain_keys = chain_pages * PAGE_SIZE

    def page_copies(bb, sl, p):
        page = pt_ref[bb, p]
        return (pltpu.make_async_copy(ckv_hbm.at[layer, page], ckv_buf.at[sl, p], sems.at[0, sl]),
                pltpu.make_async_copy(krt_hbm.at[layer, page],
                                      krt_buf.at[sl, :, pl.ds(p * PAGE_SIZE, PAGE_SIZE)], sems.at[1, sl]))

    def start_all(bb, sl):
        for p in range(n_pages):
            for cp in page_copies(bb, sl, p):
                cp.start()

    @pl.when(b == 0)
    def _():
        start_all(b, slot)

    @pl.when(b + 1 < nb)
    def _():
        start_all(b + 1, 1 - slot)

    for p in range(n_pages):
        for cp in page_copies(b, slot, p):
            cp.wait()

    q = q_ref[...]
    q_lat = q[:, :kv_lora]
    q_rope = q[:, kv_lora:kv_lora + QK_ROPE]
    parts = []
    for c in range(n_chains):
        ckv = ckv_buf[slot, c * chain_pages:(c + 1) * chain_pages].reshape(chain_keys, kv_lora).astype(BF16)
        krt = krt_buf[slot, :, c * chain_keys:(c + 1) * chain_keys].astype(BF16)
        parts.append(_softmax_part(_scores(q_lat, ckv) + _dot(q_rope, krt), ckv))

    knew = jnp.concatenate([knew_ref[...], jnp.zeros((LANES - steps, QK_PAD), F32)], axis=0).astype(BF16)
    r = lax.broadcasted_iota(jnp.int32, (rows, LANES), 0) & (steps - 1)
    c = lax.broadcasted_iota(jnp.int32, (rows, LANES), 1)
    parts.append(_softmax_part(jnp.where(c <= r, _scores(q, knew), -jnp.inf), knew[:, :kv_lora]))

    m = functools.reduce(jnp.maximum, [pm for pm, _, _ in parts])
    scales = [jnp.exp(pm - m) for pm, _, _ in parts]
    l = sum(sc * pl_ for sc, (_, pl_, _) in zip(scales, parts))
    acc = sum(sc * pa for sc, (_, _, pa) in zip(scales, parts))
    o = acc / l
    y = _dot(o[0:steps].astype(BF16), wuv_ref[0])
    for h in range(1, MLA_HEADS):
        y = y + _dot(o[h * steps:(h + 1) * steps].astype(BF16), wuv_ref[h])
    y_ref[...] = _rms(y, on_ref[...]).astype(y_ref.dtype)


def _attn_sample(page_table, q, knew, cache_c, cache_rt, w, *, layer, nb, steps, n_chains):
    kv_lora = w['kv_lora']
    d_mla = MLA_HEADS * V_DIM
    rows = MLA_HEADS * steps
    n_pages = page_table.shape[1]
    assert n_pages % n_chains == 0 and steps & (steps - 1) == 0 and steps <= LANES
    assert cache_c.shape[2:] == (PAGE_SIZE, kv_lora) and cache_rt.shape[2:] == (QK_ROPE, PAGE_SIZE)
    kern = functools.partial(_attn_sample_kernel, layer=layer, steps=steps, n_pages=n_pages, n_chains=n_chains,
                             kv_lora=kv_lora)
    grid_spec = pltpu.PrefetchScalarGridSpec(
        num_scalar_prefetch=1,
        grid=(nb,),
        in_specs=[pl.BlockSpec((None, rows, QK_PAD), lambda b, pt: (b, 0, 0)),
                  pl.BlockSpec((None, steps, QK_PAD), lambda b, pt: (b, 0, 0)),
                  pl.BlockSpec(w['wuv'].shape, lambda b, pt: (0, 0, 0)),
                  pl.BlockSpec(w['onm'].shape, lambda b, pt: (0, 0)),
                  pl.BlockSpec(memory_space=pl.ANY),
                  pl.BlockSpec(memory_space=pl.ANY)],
        out_specs=pl.BlockSpec((None, steps, d_mla), lambda b, pt: (b, 0, 0)),
        scratch_shapes=[pltpu.VMEM((2, n_pages, PAGE_SIZE, kv_lora), F32),
                        pltpu.VMEM((2, QK_ROPE, n_pages * PAGE_SIZE), F32),
                        pltpu.SemaphoreType.DMA((2, 2))],
    )
    return pl.pallas_call(
        kern,
        grid_spec=grid_spec,
        out_shape=jax.ShapeDtypeStruct((nb, steps, d_mla), F32),
        compiler_params=pltpu.CompilerParams(dimension_semantics=("arbitrary",), vmem_limit_bytes=VMEM_LIMIT),
        name="attn_sample",
    )(page_table, q, knew, w['wuv'], w['onm'], cache_c, cache_rt)


def _back_kernel(x1_ref, yl_ref, ym_ref, wol_ref, wom_ref, n2_ref, wg_ref, wu_ref, wd_ref, nf_ref, y_ref):
    x2 = x1_ref[...] + (_dot(yl_ref[...], wol_ref[...]) + _dot(ym_ref[...], wom_ref[...]))
    xn = _rms(x2, n2_ref[...]).astype(BF16)
    x3 = x2 + 0.5 * _swiglu(xn, wg_ref, wu_ref, wd_ref)
    y_ref[...] = _rms(x3, nf_ref[...])


def _back(x1, yl, ym, w, *, tm):
    n, d = x1.shape
    row = lambda width: pl.BlockSpec((tm, width), lambda i: (i, 0))
    consts = [w['wol'], w['wom'], w['n2'], w['wg2'], w['wu2'], w['wd2'], w['nf']]
    return pl.pallas_call(
        _back_kernel,
        grid=(n // tm,),
        in_specs=[row(d), row(yl.shape[1]), row(ym.shape[1])] + [_const_spec(c.shape) for c in consts],
        out_specs=row(d),
        out_shape=jax.ShapeDtypeStruct((n, d), F32),
        compiler_params=pltpu.CompilerParams(dimension_semantics=("arbitrary",), vmem_limit_bytes=VMEM_LIMIT),
        name="back",
    )(x1, yl, ym, *consts)


def _rot_cols(wr):
    half = wr.shape[-1] // 2
    return jnp.concatenate([-wr[..., half:], wr[..., :half]], axis=-1)


def _block_diag(wb):
    n, di, do = wb.shape
    eye = jnp.eye(n, dtype=wb.dtype)
    return (wb[:, :, None, :] * eye[:, None, :, None]).reshape(n * di, n * do)


def _prep_weights(l, ffn1_norm, ffn1_w_gate, ffn1_w_up, ffn1_w_down, mix_norm, w_in, conv_w, conv_b,
                  lru_w_a, lru_b_a, lru_w_i, lru_b_i, lru_lambda, q_norm, w_q_up, kv_norm, w_uk, w_uv,
                  out_norm_lru, out_norm_mla, w_out, ffn2_norm, ffn2_w_gate, ffn2_w_up, ffn2_w_down,
                  final_norm):
    d_lru = conv_w.shape[-1]
    q_lora = q_norm.shape[-1]
    kv_lora = kv_norm.shape[-1]
    d_model = w_in.shape[1]
    o4 = 2 * d_lru + q_lora + kv_lora
    win = w_in[l]
    wkr = win[:, o4:]
    win_ext = jnp.concatenate(
        [win[:, :o4], wkr, _rot_cols(wkr), jnp.zeros((d_model, LANES - 2 * QK_ROPE), F32)], axis=1)
    wq = w_q_up[l].reshape(q_lora, MLA_HEADS, QK_NOPE + QK_ROPE)
    wq_rope = wq[:, :, QK_NOPE:]
    wq_ext = jnp.concatenate([wq[:, :, :QK_NOPE].reshape(q_lora, -1), wq_rope.reshape(q_lora, -1),
                              _rot_cols(wq_rope).reshape(q_lora, -1)], axis=1)
    wuk_bd = _block_diag(jnp.transpose(w_uk[l], (1, 2, 0)))
    eye = jnp.eye(MLA_HEADS, dtype=F32)
    wuv_ext = (jnp.transpose(w_uv[l], (1, 0, 2))[:, :, None, :] * eye[:, None, :, None]).reshape(
        MLA_HEADS, kv_lora, MLA_HEADS * V_DIM)
    row = lambda v: v.reshape(1, -1)
    return {
        'd_lru': d_lru, 'q_lora': q_lora, 'kv_lora': kv_lora,
        'n1': row(ffn1_norm[l]), 'wg1': ffn1_w_gate[l].astype(BF16), 'wu1': ffn1_w_up[l].astype(BF16),
        'wd1': ffn1_w_down[l].astype(BF16), 'nm': row(mix_norm[l]), 'win': win_ext.astype(BF16),
        'qn': row(q_norm[l]), 'wq': wq_ext.astype(BF16), 'kvn': row(kv_norm[l]), 'wuk': wuk_bd.astype(BF16),
        'cw': conv_w[l], 'cb': row(conv_b[l]),
        'wai': jnp.concatenate([_block_diag(lru_w_a[l]), _block_diag(lru_w_i[l])], axis=1).astype(BF16),
        'bai': row(jnp.concatenate([lru_b_a[l], lru_b_i[l]])), 'lam': row(lru_lambda[l]),
        'onl': row(out_norm_lru[l]), 'wuv': wuv_ext.astype(BF16),
        'wuvt': jnp.transpose(wuv_ext, (0, 2, 1)).astype(BF16), 'onm': row(out_norm_mla[l]),
        'wol': w_out[l, :d_lru].astype(BF16), 'wom': w_out[l, d_lru:].astype(BF16),
        'n2': row(ffn2_norm[l]), 'wg2': ffn2_w_gate[l].astype(BF16), 'wu2': ffn2_w_up[l].astype(BF16),
        'wd2': ffn2_w_down[l].astype(BF16), 'nf': row(final_norm),
    }


def _rope_tables(pos):
    half = QK_ROPE // 2
    freqs = ROPE_BASE ** (-jnp.arange(half, dtype=F32) / half)
    ang = pos.astype(F32)[:, None] * freqs[None, :]
    return jnp.tile(jnp.cos(ang), (1, 2 * MLA_HEADS)), jnp.tile(jnp.sin(ang), (1, 2 * MLA_HEADS))


FRONT_ROWS = 256
BACK_ROWS = 256
LRU_ROWS = 512
ATTN_ROWS = FRONT_ROWS
SAMPLE_CHAINS = 4


def kernel(x_prompt, x_sample, cache_kv_latent, cache_k_rope, state_lru_h, state_conv, page_table, ffn1_norm, ffn1_w_gate, ffn1_w_up, ffn1_w_down, mix_norm, w_in, conv_w, conv_b, lru_w_a, lru_b_a, lru_w_i, lru_b_i, lru_lambda, q_norm, w_q_up, kv_norm, w_uk, w_uv, out_norm_lru, out_norm_mla, w_out, ffn2_norm, ffn2_w_gate, ffn2_w_up, ffn2_w_down, final_norm):
    B, S, D = x_prompt.shape
    DB, DS, _ = x_sample.shape
    depth = ffn1_norm.shape[0]
    assert depth == 1 and S >= CONV_W - 1 and DS >= CONV_W - 1
    past_len = page_table.shape[1] * PAGE_SIZE
    l = 0
    w = _prep_weights(l, ffn1_norm, ffn1_w_gate, ffn1_w_up, ffn1_w_down, mix_norm, w_in, conv_w, conv_b,
                      lru_w_a, lru_b_a, lru_w_i, lru_b_i, lru_lambda, q_norm, w_q_up, kv_norm, w_uk, w_uv,
                      out_norm_lru, out_norm_mla, w_out, ffn2_norm, ffn2_w_gate, ffn2_w_up, ffn2_w_down,
                      final_norm)
    d_lru, kv_lora = w['d_lru'], w['kv_lora']
    hist = CONV_W - 1

    cos_p, sin_p = _rope_tables(jnp.arange(S, dtype=jnp.int32))
    x1, xg, qt, kcat, vt, kv_p, kr_p = _front(x_prompt.reshape(B * S, D), cos_p, sin_p, S // FRONT_ROWS, w,
                                              tm=FRONT_ROWS, kcat_dtype=BF16)
    yl, h_p, conv_p = _lru_prompt(xg, jnp.zeros((B, hist, d_lru), F32), jnp.zeros((B, d_lru), F32), w,
                                  batch=B, seq=S, ts=LRU_ROWS)
    ym = _attn_prompt(qt, kcat, vt, w, batch=B, seq=S, tq=ATTN_ROWS)
    y_p = _back(x1, yl, ym.reshape(B * S, -1), w, tm=BACK_ROWS).reshape(B, S, D)

    ns = DS * DB
    cos_s, sin_s = _rope_tables(jnp.repeat(past_len + jnp.arange(DS, dtype=jnp.int32), DB))
    xs = jnp.transpose(x_sample, (1, 0, 2)).reshape(ns, D)
    x1s, xgs, qts, kcats, _, kv_s, kr_s = _front(xs, cos_s, sin_s, ns // FRONT_ROWS, w, tm=FRONT_ROWS,
                                                 kcat_dtype=F32)
    yls, h_s, conv_s = _lru_sample(xgs, jnp.transpose(state_conv[l], (1, 0, 2)), state_lru_h[l], w,
                                   steps=DS, nb=DB)
    qs = jnp.transpose(qts.reshape(ns // FRONT_ROWS, QK_PAD, MLA_HEADS, FRONT_ROWS), (2, 0, 3, 1))
    q_b = jnp.transpose(qs.reshape(MLA_HEADS, DS, DB, QK_PAD), (2, 0, 1, 3)).reshape(DB, MLA_HEADS * DS, QK_PAD)
    knew_b = jnp.transpose(kcats.reshape(DS, DB, QK_PAD), (1, 0, 2))
    yms = _attn_sample(page_table, q_b, knew_b, cache_kv_latent, jnp.transpose(cache_k_rope, (0, 1, 3, 2)), w,
                       layer=l, nb=DB, steps=DS, n_chains=SAMPLE_CHAINS)
    yms_t = jnp.transpose(yms, (1, 0, 2)).reshape(ns, -1).astype(BF16)
    y_s = _back(x1s, yls.reshape(ns, d_lru), yms_t, w, tm=BACK_ROWS)
    to_b = lambda a: jnp.transpose(a.reshape(DS, DB, -1), (1, 0, 2))

    return (y_p, to_b(y_s),
            kv_p.reshape(1, B, S, kv_lora), kr_p.reshape(1, B, S, QK_ROPE),
            h_p[None], conv_p[None],
            to_b(kv_s)[None], to_b(kr_s)[None],
            h_s[None], jnp.transpose(conv_s, (1, 0, 2))[None])
```
